```python
import math
import jax, jax.numpy as jnp
from jax import lax
import numpy as np

D_MODEL = 2048
BATCH = 1
SEQ = 8192
DEPTH = 2

EPS = 1e-6
CONV_DIM = D_MODEL // 2
CONV_WIDTH = 31
SGU_DIM = D_MODEL // 2
SGU_GROUPS = 8
SGU_GROUP_DIM = SGU_DIM // SGU_GROUPS
CHUNK = 128
SB_HEADS = 16
SB_HEAD_DIM = 64
SB_DIM = SB_HEADS * SB_HEAD_DIM
Q_BLOCK = 128
N_BRANCH = 3
A_COLS = 2 * CONV_DIM
B_COLS = 2 * SGU_DIM
C_COLS = 3 * SB_DIM
G_COLS = N_BRANCH * D_MODEL
IN_DIM = A_COLS + B_COLS + C_COLS + G_COLS
D_FF = 4 * D_MODEL

kernel_name = "hybrid_conv_sgu_stickbreaking_gated_block"


def rms_norm(x, g):
    xf = x.astype(jnp.float32)
    var = jnp.mean(xf * xf, axis=-1, keepdims=True)
    return (xf * lax.rsqrt(var + EPS) * g.astype(jnp.float32)).astype(x.dtype)


def layer_norm(x, g, b):
    xf = x.astype(jnp.float32)
    mu = jnp.mean(xf, axis=-1, keepdims=True)
    var = jnp.mean(jnp.square(xf - mu), axis=-1, keepdims=True)
    y = (xf - mu) * lax.rsqrt(var + EPS) * g.astype(jnp.float32) + b.astype(jnp.float32)
    return y.astype(x.dtype)


def conformer_conv(pa, conv_w, conv_b, ln_g, ln_b):
    val, gate = jnp.split(pa, 2, axis=-1)
    z = val * jax.nn.sigmoid(gate)
    z = lax.conv_general_dilated(
        z, conv_w[:, None, :].astype(z.dtype),
        window_strides=(1,), padding=[(CONV_WIDTH - 1, 0)],
        dimension_numbers=("NWC", "WIO", "NWC"),
        feature_group_count=CONV_DIM) + conv_b
    z = layer_norm(z, ln_g, ln_b)
    return jax.nn.silu(z)


def spatial_gating(pb, ln_g, ln_b, w_s, b_s):
    bsz, seq, _ = pb.shape
    uv = jax.nn.gelu(pb)
    u, v = jnp.split(uv, 2, axis=-1)
    v = layer_norm(v, ln_g, ln_b)
    v = v.reshape(bsz, seq // CHUNK, CHUNK, SGU_GROUPS, SGU_GROUP_DIM)
    tri = jnp.tril(jnp.ones((CHUNK, CHUNK), dtype=bool))
    w = jnp.where(tri[None], w_s, jnp.zeros_like(w_s))
    s = jnp.einsum("gts,bnsgc->bntgc", w.astype(v.dtype), v)
    s = s + jnp.transpose(b_s)[None, None, :, :, None]
    return u * s.reshape(bsz, seq, SGU_DIM)


def stick_breaking_attention(q, k, v):
    seq = q.shape[2]
    scale = 1.0 / math.sqrt(SB_HEAD_DIM)
    outs = []
    for i in range(seq // Q_BLOCK):
        q0, kend = i * Q_BLOCK, (i + 1) * Q_BLOCK
        qb = q[:, :, q0:kend]
        kb = k[:, :, :kend]
        vb = v[:, :, :kend]
        z = jnp.einsum("bhtd,bhsd->bhts", qb, kb).astype(jnp.float32) * scale
        t_pos = q0 + jnp.arange(Q_BLOCK)
        s_pos = jnp.arange(kend)
        causal = s_pos[None, :] < t_pos[:, None]
        log_beta = jax.nn.log_sigmoid(z)
        log_1mb = jnp.where(causal, jax.nn.log_sigmoid(-z), 0.0)
        suffix = lax.cumsum(log_1mb, axis=3, reverse=True) - log_1mb
        a = jnp.where(causal, jnp.exp(log_beta + suffix), 0.0)
        outs.append(jnp.einsum("bhts,bhsd->bhtd", a.astype(vb.dtype), vb))
    return jnp.concatenate(outs, axis=2)


def setup_inputs(seed: int = 0) -> dict:
    key = jax.random.key(seed)
    ks = jax.random.split(key, 20)
    f32 = jnp.float32
    nrm = lambda k, shape, s: jax.random.normal(k, shape, f32) * s
    return {
        "x": jax.random.normal(ks[0], (BATCH, SEQ, D_MODEL), f32),
        "attn_norm_g": 1.0 + nrm(ks[1], (DEPTH, D_MODEL), 0.01),
        "w_in": nrm(ks[2], (DEPTH, D_MODEL, IN_DIM), D_MODEL ** -0.5),
        "b_gate": nrm(ks[3], (DEPTH, G_COLS), 0.02),
        "conv_w": nrm(ks[4], (DEPTH, CONV_WIDTH, CONV_DIM), CONV_WIDTH ** -0.5),
        "conv_b": nrm(ks[5], (DEPTH, CONV_DIM), 0.02),
        "conv_ln_g": 1.0 + nrm(ks[6], (DEPTH, CONV_DIM), 0.01),
        "conv_ln_b": nrm(ks[7], (DEPTH, CONV_DIM), 0.02),
        "sgu_ln_g": 1.0 + nrm(ks[8], (DEPTH, SGU_DIM), 0.01),
        "sgu_ln_b": nrm(ks[9], (DEPTH, SGU_DIM), 0.02),
        "sgu_w": nrm(ks[10], (DEPTH, SGU_GROUPS, CHUNK, CHUNK), CHUNK ** -0.5),
        "sgu_b": 1.0 + nrm(ks[11], (DEPTH, SGU_GROUPS, CHUNK), 0.02),
        "w_out_conv": nrm(ks[12], (DEPTH, CONV_DIM, D_MODEL), CONV_DIM ** -0.5),
        "w_out_sgu": nrm(ks[13], (DEPTH, SGU_DIM, D_MODEL), SGU_DIM ** -0.5),
        "w_out_sb": nrm(ks[14], (DEPTH, SB_DIM, D_MODEL), SB_DIM ** -0.5),
        "w_o": nrm(ks[15], (DEPTH, D_MODEL, D_MODEL), D_MODEL ** -0.5),
        "mlp_norm_g": 1.0 + nrm(ks[16], (DEPTH, D_MODEL), 0.01),
        "w_ff1": nrm(ks[17], (DEPTH, D_MODEL, D_FF), D_MODEL ** -0.5),
        "w_ff2": nrm(ks[18], (DEPTH, D_FF, D_MODEL), D_FF ** -0.5),
        "final_norm_g": 1.0 + nrm(ks[19], (D_MODEL,), 0.01),
    }


def reference(x, attn_norm_g, w_in, b_gate, conv_w, conv_b, conv_ln_g, conv_ln_b,
              sgu_ln_g, sgu_ln_b, sgu_w, sgu_b, w_out_conv, w_out_sgu, w_out_sb,
              w_o, mlp_norm_g, w_ff1, w_ff2, final_norm_g):
    bsz, seq, _ = x.shape
    for l in range(DEPTH):
        h = rms_norm(x, attn_norm_g[l])
        p = jnp.einsum("bsd,de->bse", h, w_in[l])
        pa = p[..., :A_COLS]
        pb = p[..., A_COLS:A_COLS + B_COLS]
        pc = p[..., A_COLS + B_COLS:A_COLS + B_COLS + C_COLS]
        pg = p[..., A_COLS + B_COLS + C_COLS:]

        ya = conformer_conv(pa, conv_w[l], conv_b[l], conv_ln_g[l], conv_ln_b[l])
        ya = jnp.einsum("bsc,cd->bsd", ya, w_out_conv[l])

        yb = spatial_gating(pb, sgu_ln_g[l], sgu_ln_b[l], sgu_w[l], sgu_b[l])
        yb = jnp.einsum("bsc,cd->bsd", yb, w_out_sgu[l])

        qkv = pc.reshape(bsz, seq, 3, SB_HEADS, SB_HEAD_DIM)
        q, k, v = (jnp.transpose(qkv[:, :, j], (0, 2, 1, 3)) for j in range(3))
        yc = stick_breaking_attention(q, k, v)
        yc = jnp.transpose(yc, (0, 2, 1, 3)).reshape(bsz, seq, SB_DIM)
        yc = jnp.einsum("bsc,cd->bsd", yc, w_out_sb[l])

        gates = jax.nn.sigmoid(pg + b_gate[l]).reshape(bsz, seq, N_BRANCH, D_MODEL)
        mixed = gates[:, :, 0] * ya + gates[:, :, 1] * yb + gates[:, :, 2] * yc
        x = x + jnp.einsum("bsd,de->bse", mixed, w_o[l])

        h = rms_norm(x, mlp_norm_g[l])
        f = jnp.square(jax.nn.relu(jnp.einsum("bsd,df->bsf", h, w_ff1[l])))
        x = x + jnp.einsum("bsf,fd->bsd", f, w_ff2[l])
    return rms_norm(x, final_norm_g)
```

```python
import functools
import math

import jax
import jax.numpy as jnp
from jax import lax
from jax.experimental import pallas as pl
from jax.experimental.pallas import tpu as pltpu

EPS = 1e-6
CONV_WIDTH = 31
CHUNK = 128
SGU_GROUPS = 8
SB_HEAD_DIM = 64
N_BRANCH = 3

LANES = 128
VMEM_LIMIT = 48 * 1024 * 1024
F32_EXP_ZERO_BELOW = -110.0

f32 = jnp.float32
bf16 = jnp.bfloat16


def _cparams(sem):
    return pltpu.CompilerParams(dimension_semantics=sem, vmem_limit_bytes=VMEM_LIMIT)


def _rmsnorm_kernel(x_ref, g_ref, o_ref):
    x = x_ref[...]
    var = jnp.mean(x * x, axis=-1, keepdims=True)
    o_ref[...] = (x * lax.rsqrt(var + EPS) * g_ref[...]).astype(o_ref.dtype)


def rmsnorm(x, g, out_dtype, tm=512):
    s, d = x.shape
    return pl.pallas_call(
        _rmsnorm_kernel,
        grid=(s // tm,),
        in_specs=[pl.BlockSpec((tm, d), lambda i: (i, 0)),
                  pl.BlockSpec((1, d), lambda i: (0, 0))],
        out_specs=pl.BlockSpec((tm, d), lambda i: (i, 0)),
        out_shape=jax.ShapeDtypeStruct((s, d), out_dtype),
        compiler_params=_cparams(("parallel",)),
        name="rmsnorm",
    )(x, g.reshape(1, d))


def _mm_kernel(*refs, n_w, n_extra, nk, epilogue):
    a_ref = refs[0]
    w_refs = refs[1:1 + n_w]
    extra_refs = refs[1 + n_w:1 + n_w + n_extra]
    o_ref = refs[1 + n_w + n_extra]
    acc_refs = refs[2 + n_w + n_extra:]
    j = pl.program_id(1)

    if nk == 1:
        accs = [jnp.dot(a_ref[...], w[...], preferred_element_type=f32) for w in w_refs]
        o_ref[...] = epilogue(accs, [e[...] for e in extra_refs], j).astype(o_ref.dtype)
        return

    k = pl.program_id(2)

    @pl.when(k == 0)
    def _():
        for acc in acc_refs:
            acc[...] = jnp.zeros_like(acc)

    a = a_ref[...]
    for w, acc in zip(w_refs, acc_refs):
        acc[...] += jnp.dot(a, w[...], preferred_element_type=f32)

    @pl.when(k == nk - 1)
    def _():
        o_ref[...] = epilogue([acc[...] for acc in acc_refs],
                              [e[...] for e in extra_refs], j).astype(o_ref.dtype)


def matmul(a, ws, n_out, epilogue, out_dtype, extras=(), tm=1024, tn=512, tk=2048, name="mm"):
    m, kdim = a.shape
    tm = min(tm, m)
    tk = min(tk, kdim)
    nk = kdim // tk
    grid = (m // tm, n_out // tn, nk)
    in_specs = [pl.BlockSpec((tm, tk), lambda i, j, k: (i, k))]
    args = [a]
    for w, off in ws:
        ob = off // tn
        in_specs.append(pl.BlockSpec((tk, tn), lambda i, j, k, ob=ob: (k, j + ob)))
        args.append(w)
    for e, off, kind in extras:
        ob = off // tn
        if kind == "tile":
            in_specs.append(pl.BlockSpec((tm, tn), lambda i, j, k, ob=ob: (i, j + ob)))
        else:
            in_specs.append(pl.BlockSpec((1, tn), lambda i, j, k, ob=ob: (0, j + ob)))
        args.append(e)
    scratch = [pltpu.VMEM((tm, tn), f32) for _ in ws] if nk > 1 else []
    kern = functools.partial(_mm_kernel, n_w=len(ws), n_extra=len(extras), nk=nk, epilogue=epilogue)
    return pl.pallas_call(
        kern,
        grid=grid,
        in_specs=in_specs,
        out_specs=pl.BlockSpec((tm, tn), lambda i, j, k: (i, j)),
        out_shape=jax.ShapeDtypeStruct((m, n_out), out_dtype),
        scratch_shapes=scratch,
        compiler_params=_cparams(("parallel", "parallel", "arbitrary")),
        name=name,
    )(*args)


def _ep_glu(accs, extras, j):
    return accs[0] * jax.nn.sigmoid(accs[1])


def _ep_gelu(accs, extras, j):
    return jax.nn.gelu(accs[0])


def _ep_qkv(accs, extras, j, q_tiles):
    scale = jnp.where(j < q_tiles, 1.0 / math.sqrt(SB_HEAD_DIM), 1.0).astype(f32)
    return accs[0] * scale


def _ep_gates(accs, extras, j):
    return jax.nn.sigmoid(accs[0] + extras[0])


def _ep_residual(accs, extras, j):
    return extras[0] + accs[0]


def _ep_relu2(accs, extras, j):
    return jnp.square(jnp.maximum(accs[0], 0.0))


HALO = 32


def _conv_kernel(z_ref, halo_ref, cw_ref, cb_ref, g_ref, b_ref, o_ref, buf_ref, y_ref, *, tm, c):
    i = pl.program_id(0)
    halo = halo_ref[...]
    buf_ref[0:HALO, :] = jnp.where(i == 0, jnp.zeros_like(halo), halo)
    buf_ref[HALO:HALO + tm, :] = z_ref[...]
    base = HALO - (CONV_WIDTH - 1)

    def chan_block(cb, carry):
        c0 = pl.multiple_of(cb * LANES, LANES)
        acc = jnp.zeros((tm, LANES), f32)
        for w in range(CONV_WIDTH):
            acc = acc + buf_ref[pl.ds(base + w, tm), pl.ds(c0, LANES)] * cw_ref[pl.ds(w, 1), pl.ds(c0, LANES)]
        y_ref[:, pl.ds(c0, LANES)] = acc
        return carry

    lax.fori_loop(0, c // LANES, chan_block, 0)
    y = y_ref[...] + cb_ref[...]
    mu = jnp.mean(y, axis=-1, keepdims=True)
    yc = y - mu
    var = jnp.mean(yc * yc, axis=-1, keepdims=True)
    yn = yc * lax.rsqrt(var + EPS) * g_ref[...] + b_ref[...]
    o_ref[...] = (yn * jax.nn.sigmoid(yn)).astype(o_ref.dtype)


def conv_branch(z, conv_w, conv_b, ln_g, ln_b, tm=256):
    s, c = z.shape
    cwp = jnp.zeros((HALO, c), f32).at[:CONV_WIDTH].set(conv_w)
    row = lambda v: v.reshape(1, c)
    kern = functools.partial(_conv_kernel, tm=tm, c=c)
    return pl.pallas_call(
        kern,
        grid=(s // tm,),
        in_specs=[pl.BlockSpec((tm, c), lambda i: (i, 0)),
                  pl.BlockSpec((HALO, c), lambda i: (jnp.maximum(i * (tm // HALO) - 1, 0), 0)),
                  pl.BlockSpec((HALO, c), lambda i: (0, 0)),
                  pl.BlockSpec((1, c), lambda i: (0, 0)),
                  pl.BlockSpec((1, c), lambda i: (0, 0)),
                  pl.BlockSpec((1, c), lambda i: (0, 0))],
        out_specs=pl.BlockSpec((tm, c), lambda i: (i, 0)),
        out_shape=jax.ShapeDtypeStruct((s, c), bf16),
        scratch_shapes=[pltpu.VMEM((HALO + tm, c), f32), pltpu.VMEM((tm, c), f32)],
        compiler_params=_cparams(("parallel",)),
        name="conv_branch",
    )(z, z, cwp, row(conv_b), row(ln_g), row(ln_b))


def _sgu_kernel(uv_ref, g_ref, b_ref, w_ref, bs_ref, o_ref, *, tm, c):
    v = uv_ref[:, c:2 * c].astype(f32)
    mu = jnp.mean(v, axis=-1, keepdims=True)
    vc = v - mu
    var = jnp.mean(vc * vc, axis=-1, keepdims=True)
    vn = (vc * lax.rsqrt(var + EPS) * g_ref[...] + b_ref[...]).astype(bf16)
    row = lax.broadcasted_iota(jnp.int32, (CHUNK, CHUNK), 0)
    col = lax.broadcasted_iota(jnp.int32, (CHUNK, CHUNK), 1)
    tril = col <= row
    gd = c // SGU_GROUPS
    for g in range(SGU_GROUPS):
        wg = jnp.where(tril, w_ref[g], 0.0).astype(bf16)
        for ch in range(tm // CHUNK):
            r0 = ch * CHUNK
            s = jnp.dot(wg, vn[r0:r0 + CHUNK, g * gd:(g + 1) * gd], preferred_element_type=f32)
            s = s + bs_ref[g]
            u = uv_ref[r0:r0 + CHUNK, g * gd:(g + 1) * gd].astype(f32)
            o_ref[r0:r0 + CHUNK, g * gd:(g + 1) * gd] = (u * s).astype(o_ref.dtype)


def sgu_branch(uv, ln_g, ln_b, w_s, b_s, tm=256):
    s, c2 = uv.shape
    c = c2 // 2
    bs_b = jnp.broadcast_to(b_s[:, :, None], (SGU_GROUPS, CHUNK, c // SGU_GROUPS))
    kern = functools.partial(_sgu_kernel, tm=tm, c=c)
    return pl.pallas_call(
        kern,
        grid=(s // tm,),
        in_specs=[pl.BlockSpec((tm, c2), lambda i: (i, 0)),
                  pl.BlockSpec((1, c), lambda i: (0, 0)),
                  pl.BlockSpec((1, c), lambda i: (0, 0)),
                  pl.BlockSpec((SGU_GROUPS, CHUNK, CHUNK), lambda i: (0, 0, 0)),
                  pl.BlockSpec((SGU_GROUPS, CHUNK, c // SGU_GROUPS), lambda i: (0, 0, 0))],
        out_specs=pl.BlockSpec((tm, c), lambda i: (i, 0)),
        out_shape=jax.ShapeDtypeStruct((s, c), bf16),
        compiler_params=_cparams(("parallel",)),
        name="sgu_branch",
    )(uv, ln_g.reshape(1, c), ln_b.reshape(1, c), w_s, bs_b)


TQ = 128
TK = 128


def _sb_kernel(q_ref, k_ref, v_ref, o_ref):
    i = pl.program_id(1)
    q = q_ref[...]
    lane = lax.broadcasted_iota(jnp.int32, (TQ, LANES), 1)
    low = lane < SB_HEAD_DIM
    qh = (jnp.where(low, q, jnp.zeros_like(q)), jnp.where(low, jnp.zeros_like(q), q))

    r2 = lax.broadcasted_iota(jnp.int32, (2 * TK, 2 * TK), 0)
    c2 = lax.broadcasted_iota(jnp.int32, (2 * TK, 2 * TK), 1)
    rk = jnp.where(r2 >= TK, r2 - TK, r2)
    u2 = jnp.where((c2 >= TK) | (rk > c2), 1.0, 0.0).astype(bf16)

    row = lax.broadcasted_iota(jnp.int32, (TQ, TK), 0)
    col = lax.broadcasted_iota(jnp.int32, (TQ, TK), 1)
    causal = col < row

    def tile(qm, kblk, vblk, carry, diag):
        z = lax.dot_general(qm, kblk, (((1,), (1,)), ((), ())), preferred_element_type=f32)
        sp = jnp.log1p(jnp.exp(-jnp.abs(z)))
        log_beta = jnp.minimum(z, 0.0) - sp
        log_1mb = -jnp.maximum(z, 0.0) - sp
        if diag:
            log_1mb = jnp.where(causal, log_1mb, 0.0)
        hi = log_1mb.astype(bf16)
        lo = (log_1mb - hi.astype(f32)).astype(bf16)
        t = jnp.dot(jnp.concatenate([hi, lo], axis=1), u2, preferred_element_type=f32)
        a = jnp.exp(log_beta + t[:, :TK] + carry)
        if diag:
            a = jnp.where(causal, a, 0.0)
        pv = jnp.dot(a.astype(bf16), vblk, preferred_element_type=f32)
        return pv, carry + t[:, TK:]

    def both_heads(kblk, vblk, c0, c1, diag):
        pv0, c0 = tile(qh[0], kblk, vblk, c0, diag)
        pv1, c1 = tile(qh[1], kblk, vblk, c1, diag)
        return jnp.where(low, pv0, pv1), c0, c1

    def block(j):
        r0 = pl.multiple_of(j * TK, TK)
        return k_ref[pl.ds(r0, TK), :], v_ref[pl.ds(r0, TK), :]

    kblk, vblk = block(i)
    zero = jnp.zeros((TQ, TK), f32)
    acc, c0, c1 = both_heads(kblk, vblk, zero, zero, True)

    def cond(st):
        j, cmax = st[0], st[1]
        return jnp.logical_and(j >= 0, cmax > F32_EXP_ZERO_BELOW)

    def body(st):
        j, _, acc, c0, c1 = st
        kblk, vblk = block(j)
        pv, c0, c1 = both_heads(kblk, vblk, c0, c1, False)
        return j - 1, jnp.max(jnp.maximum(c0, c1)), acc + pv, c0, c1

    st = lax.while_loop(cond, body, (i - 1, jnp.max(jnp.maximum(c0, c1)), acc, c0, c1))
    o_ref[...] = st[2].astype(o_ref.dtype)


def sb_attention(pc, n_heads):
    s = pc.shape[0]
    npair = n_heads * SB_HEAD_DIM // LANES
    return pl.pallas_call(
        _sb_kernel,
        grid=(npair, s // TQ),
        in_specs=[pl.BlockSpec((TQ, LANES), lambda hp, i: (i, hp)),
                  pl.BlockSpec((s, LANES), lambda hp, i: (0, npair + hp)),
                  pl.BlockSpec((s, LANES), lambda hp, i: (0, 2 * npair + hp))],
        out_specs=pl.BlockSpec((TQ, LANES), lambda hp, i: (i, hp)),
        out_shape=jax.ShapeDtypeStruct((s, npair * LANES), bf16),
        compiler_params=_cparams(("parallel", "arbitrary")),
        name="sb_attention",
    )(pc, pc, pc)


def _merge_kernel(a_ref, b_ref, c_ref, wa_ref, wb_ref, wc_ref, g0_ref, g1_ref, g2_ref, o_ref):
    ya = jnp.dot(a_ref[...], wa_ref[...], preferred_element_type=f32)
    yb = jnp.dot(b_ref[...], wb_ref[...], preferred_element_type=f32)
    yc = jnp.dot(c_ref[...], wc_ref[...], preferred_element_type=f32)
    mixed = (g0_ref[...].astype(f32) * ya + g1_ref[...].astype(f32) * yb
             + g2_ref[...].astype(f32) * yc)
    o_ref[...] = mixed.astype(o_ref.dtype)


def gated_merge(a, b, c, wa, wb, wc, gates, tm=1024, tn=512):
    s, kd = a.shape
    tm = min(tm, s)
    d = wa.shape[1]
    act = pl.BlockSpec((tm, kd), lambda i, j: (i, 0))
    wsp = pl.BlockSpec((kd, tn), lambda i, j: (0, j))
    gsp = lambda b_: pl.BlockSpec((tm, tn), lambda i, j, b_=b_: (i, j + b_ * (d // tn)))
    return pl.pallas_call(
        _merge_kernel,
        grid=(s // tm, d // tn),
        in_specs=[act, act, act, wsp, wsp, wsp, gsp(0), gsp(1), gsp(2)],
        out_specs=pl.BlockSpec((tm, tn), lambda i, j: (i, j)),
        out_shape=jax.ShapeDtypeStruct((s, d), bf16),
        compiler_params=_cparams(("parallel", "parallel")),
        name="gated_merge",
    )(a, b, c, wa, wb, wc, gates, gates, gates)


def kernel(x, attn_norm_g, w_in, b_gate, conv_w, conv_b, conv_ln_g, conv_ln_b, sgu_ln_g, sgu_ln_b,
           sgu_w, sgu_b, w_out_conv, w_out_sgu, w_out_sb, w_o, mlp_norm_g, w_ff1, w_ff2, final_norm_g):
    bsz, seq, d = x.shape
    depth = w_in.shape[0]
    conv_dim = conv_w.shape[2]
    sgu_dim = sgu_ln_g.shape[1]
    sb_dim = w_out_sb.shape[1]
    n_heads = sb_dim // SB_HEAD_DIM
    a_cols, b_cols, c_cols = 2 * conv_dim, 2 * sgu_dim, 3 * sb_dim
    g_cols = N_BRANCH * d
    tn = 512

    outs = []
    for bi in range(bsz):
        xs = x[bi]
        for l in range(depth):
            win = w_in[l].astype(bf16)
            h = rmsnorm(xs, attn_norm_g[l], bf16)
            z = matmul(h, [(win, 0), (win, conv_dim)], conv_dim, _ep_glu, f32, tn=tn, name="proj_glu")
            uv = matmul(h, [(win, a_cols)], b_cols, _ep_gelu, bf16, tn=tn, name="proj_gelu")
            pc = matmul(h, [(win, a_cols + b_cols)], c_cols,
                        functools.partial(_ep_qkv, q_tiles=sb_dim // tn), bf16, tn=tn, name="proj_qkv")
            gates = matmul(h, [(win, a_cols + b_cols + c_cols)], g_cols, _ep_gates, bf16,
                           extras=[(b_gate[l].reshape(1, g_cols), 0, "row")], tn=tn, name="proj_gates")

            ya = conv_branch(z, conv_w[l], conv_b[l], conv_ln_g[l], conv_ln_b[l])
            yb = sgu_branch(uv, sgu_ln_g[l], sgu_ln_b[l], sgu_w[l], sgu_b[l])
            yc = sb_attention(pc, n_heads)

            mixed = gated_merge(ya, yb, yc, w_out_conv[l].astype(bf16), w_out_sgu[l].astype(bf16),
                                w_out_sb[l].astype(bf16), gates)
            xs = matmul(mixed, [(w_o[l].astype(bf16), 0)], d, _ep_residual, f32,
                        extras=[(xs, 0, "tile")], tn=tn, name="wo_residual")

            h2 = rmsnorm(xs, mlp_norm_g[l], bf16)
            f = matmul(h2, [(w_ff1[l].astype(bf16), 0)], w_ff1.shape[2], _ep_relu2, bf16, tn=tn, name="ff1")
            xs = matmul(f, [(w_ff2[l].astype(bf16), 0)], d, _ep_residual, f32,
                        extras=[(xs, 0, "tile")], tn=tn, name="ff2_residual")
        outs.append(rmsnorm(xs, final_norm_g, f32))
    return jnp.stack(outs, axis=0)
```

```python
import functools
import math

import numpy as np
import jax
import jax.numpy as jnp
from jax import lax
from jax.experimental import pallas as pl
from jax.experimental.pallas import tpu as pltpu

EPS = 1e-6
CONV_WIDTH = 31
CHUNK = 128
SGU_GROUPS = 8
SB_HEAD_DIM = 64
N_BRANCH = 3

LANES = 128
SUBLANES = 8
MXU_COLS = 256
VMEM_LIMIT = 56 * 1024 * 1024
F32_EXP_ZERO_BELOW = -110.0

f32 = jnp.float32
bf16 = jnp.bfloat16


def _cparams(sem):
    return pltpu.CompilerParams(dimension_semantics=sem, vmem_limit_bytes=VMEM_LIMIT)


def _rmsnorm_kernel(x_ref, g_ref, o_ref):
    x = x_ref[...]
    var = jnp.mean(x * x, axis=-1, keepdims=True)
    o_ref[...] = (x * lax.rsqrt(var + EPS) * g_ref[...]).astype(o_ref.dtype)


def rmsnorm(x, g, out_dtype, tm=512):
    s, d = x.shape
    return pl.pallas_call(
        _rmsnorm_kernel,
        grid=(s // tm,),
        in_specs=[pl.BlockSpec((tm, d), lambda i: (i, 0)),
                  pl.BlockSpec((1, d), lambda i: (0, 0))],
        out_specs=pl.BlockSpec((tm, d), lambda i: (i, 0)),
        out_shape=jax.ShapeDtypeStruct((s, d), out_dtype),
        compiler_params=_cparams(("parallel",)),
        name="rmsnorm",
    )(x, g.reshape(1, d))


def _mm_kernel(*refs, n_w, n_extra, nk, epilogue, chunk):
    a_ref = refs[0]
    w_refs = refs[1:1 + n_w]
    extra_refs = refs[1 + n_w:1 + n_w + n_extra]
    o_ref = refs[1 + n_w + n_extra]
    j = pl.program_id(1)

    if nk == 1:
        tn = o_ref.shape[1]
        for c0 in range(0, tn, chunk or tn):
            cols = slice(c0, c0 + (chunk or tn))
            accs = [jnp.dot(a_ref[...], w[:, cols].astype(bf16), preferred_element_type=f32)
                    for w in w_refs]
            o_ref[:, cols] = epilogue(accs, [e[:, cols] for e in extra_refs], j).astype(o_ref.dtype)
        return

    k = pl.program_id(2)

    @pl.when(k == 0)
    def _():
        o_ref[...] = extra_refs[0][...]

    o_ref[...] += jnp.dot(a_ref[...], w_refs[0][...].astype(bf16), preferred_element_type=f32)


def matmul(a, ws, n_out, epilogue, out_dtype, extras=(), tm=1024, tn=512, tk=2048, chunk=None,
           name="mm"):
    m, kdim = a.shape
    tm = min(tm, m)
    tk = min(tk, kdim)
    nk = kdim // tk
    grid = (m // tm, n_out // tn, nk)
    in_specs = [pl.BlockSpec((tm, tk), lambda i, j, k: (i, k))]
    args = [a]
    for w, layer, off in ws:
        ob = off // tn
        in_specs.append(pl.BlockSpec((None, tk, tn), lambda i, j, k, ob=ob, layer=layer: (layer, k, j + ob)))
        args.append(w)
    for e, off, kind in extras:
        ob = off // tn
        if kind == "tile":
            in_specs.append(pl.BlockSpec((tm, tn), lambda i, j, k, ob=ob: (i, j + ob)))
        else:
            in_specs.append(pl.BlockSpec((1, tn), lambda i, j, k, ob=ob: (0, j + ob)))
        args.append(e)
    if nk > 1:
        assert epilogue is _ep_residual and len(ws) == 1 and out_dtype == f32
    kern = functools.partial(_mm_kernel, n_w=len(ws), n_extra=len(extras), nk=nk, epilogue=epilogue,
                             chunk=chunk)
    return pl.pallas_call(
        kern,
        grid=grid,
        in_specs=in_specs,
        out_specs=pl.BlockSpec((tm, tn), lambda i, j, k: (i, j)),
        out_shape=jax.ShapeDtypeStruct((m, n_out), out_dtype),
        compiler_params=_cparams(("parallel", "parallel", "arbitrary")),
        name=name,
    )(*args)


def _ep_glu(accs, extras, j):
    return accs[0] * jax.nn.sigmoid(accs[1])


def _ep_gelu(accs, extras, j):
    return jax.nn.gelu(accs[0])


def _ep_qkv(accs, extras, j, q_tiles):
    scale = jnp.where(j < q_tiles, 1.0 / math.sqrt(SB_HEAD_DIM), 1.0).astype(f32)
    return accs[0] * scale


def _ep_gates(accs, extras, j):
    return jax.nn.sigmoid(accs[0] + extras[0])


def _ep_residual(accs, extras, j):
    return extras[0] + accs[0]


def _ep_relu2(accs, extras, j):
    return jnp.square(jnp.maximum(accs[0], 0.0))


HALO = 32
CONV_BASE = HALO - (CONV_WIDTH - 1)
CONV_TAP_GROUPS = -(-CONV_WIDTH // SUBLANES)
CONV_SHIFT_ROWS = (CONV_TAP_GROUPS - 1) * SUBLANES


def _conv_kernel(z_ref, halo_ref, cw_ref, cb_ref, g_ref, b_ref, o_ref, buf_ref, sh_ref, y_ref, *, tm, c):
    i = pl.program_id(0)
    halo = halo_ref[...]
    buf_ref[0:HALO, :] = jnp.where(i == 0, jnp.zeros_like(halo), halo)
    buf_ref[HALO:HALO + tm, :] = z_ref[...]

    def chan_block(cb, carry):
        c0 = pl.multiple_of(cb * LANES, LANES)
        for s in range(SUBLANES):
            n = tm + ((CONV_WIDTH - 1 - s) // SUBLANES) * SUBLANES
            sh_ref[s, 0:n, :] = buf_ref[pl.ds(CONV_BASE + s, n), pl.ds(c0, LANES)]
        acc = jnp.zeros((tm, LANES), f32)
        for w in range(CONV_WIDTH):
            a, s = divmod(w, SUBLANES)
            acc = acc + sh_ref[s, a * SUBLANES:a * SUBLANES + tm, :] * cw_ref[pl.ds(w, 1), pl.ds(c0, LANES)]
        y_ref[:, pl.ds(c0, LANES)] = acc
        return carry

    lax.fori_loop(0, c // LANES, chan_block, 0)
    y = y_ref[...] + cb_ref[...]
    mu = jnp.mean(y, axis=-1, keepdims=True)
    yc = y - mu
    var = jnp.mean(yc * yc, axis=-1, keepdims=True)
    yn = yc * lax.rsqrt(var + EPS) * g_ref[...] + b_ref[...]
    o_ref[...] = (yn * jax.nn.sigmoid(yn)).astype(o_ref.dtype)


def conv_branch(z, conv_w, conv_b, ln_g, ln_b, tm=256):
    s, c = z.shape
    cwp = jnp.zeros((HALO, c), f32).at[:CONV_WIDTH].set(conv_w)
    row = lambda v: v.reshape(1, c)
    kern = functools.partial(_conv_kernel, tm=tm, c=c)
    return pl.pallas_call(
        kern,
        grid=(s // tm,),
        in_specs=[pl.BlockSpec((tm, c), lambda i: (i, 0)),
                  pl.BlockSpec((HALO, c), lambda i: (jnp.maximum(i * (tm // HALO) - 1, 0), 0)),
                  pl.BlockSpec((HALO, c), lambda i: (0, 0)),
                  pl.BlockSpec((1, c), lambda i: (0, 0)),
                  pl.BlockSpec((1, c), lambda i: (0, 0)),
                  pl.BlockSpec((1, c), lambda i: (0, 0))],
        out_specs=pl.BlockSpec((tm, c), lambda i: (i, 0)),
        out_shape=jax.ShapeDtypeStruct((s, c), bf16),
        scratch_shapes=[pltpu.VMEM((HALO + tm, c), f32),
                        pltpu.VMEM((SUBLANES, tm + CONV_SHIFT_ROWS, LANES), f32),
                        pltpu.VMEM((tm, c), f32)],
        compiler_params=_cparams(("parallel",)),
        name="conv_branch",
    )(z, z, cwp, row(conv_b), row(ln_g), row(ln_b))


def _sgu_kernel(uv_ref, g_ref, b_ref, w_ref, bs_ref, o_ref, *, tm, c):
    v = uv_ref[:, c:2 * c].astype(f32)
    mu = jnp.mean(v, axis=-1, keepdims=True)
    vc = v - mu
    var = jnp.mean(vc * vc, axis=-1, keepdims=True)
    vn = (vc * lax.rsqrt(var + EPS) * g_ref[...] + b_ref[...]).astype(bf16)
    row = lax.broadcasted_iota(jnp.int32, (CHUNK, CHUNK), 0)
    col = lax.broadcasted_iota(jnp.int32, (CHUNK, CHUNK), 1)
    tril = col <= row
    gd = c // SGU_GROUPS
    for g in range(SGU_GROUPS):
        wg = jnp.where(tril, w_ref[g], 0.0).astype(bf16)
        for ch in range(tm // CHUNK):
            r0 = ch * CHUNK
            s = jnp.dot(wg, vn[r0:r0 + CHUNK, g * gd:(g + 1) * gd], preferred_element_type=f32)
            s = s + bs_ref[g]
            u = uv_ref[r0:r0 + CHUNK, g * gd:(g + 1) * gd].astype(f32)
            o_ref[r0:r0 + CHUNK, g * gd:(g + 1) * gd] = (u * s).astype(o_ref.dtype)


def sgu_branch(uv, ln_g, ln_b, w_s, b_s, tm=256):
    s, c2 = uv.shape
    c = c2 // 2
    bs_b = jnp.broadcast_to(b_s[:, :, None], (SGU_GROUPS, CHUNK, c // SGU_GROUPS))
    kern = functools.partial(_sgu_kernel, tm=tm, c=c)
    return pl.pallas_call(
        kern,
        grid=(s // tm,),
        in_specs=[pl.BlockSpec((tm, c2), lambda i: (i, 0)),
                  pl.BlockSpec((1, c), lambda i: (0, 0)),
                  pl.BlockSpec((1, c), lambda i: (0, 0)),
                  pl.BlockSpec((SGU_GROUPS, CHUNK, CHUNK), lambda i: (0, 0, 0)),
                  pl.BlockSpec((SGU_GROUPS, CHUNK, c // SGU_GROUPS), lambda i: (0, 0, 0))],
        out_specs=pl.BlockSpec((tm, c), lambda i: (i, 0)),
        out_shape=jax.ShapeDtypeStruct((s, c), bf16),
        compiler_params=_cparams(("parallel",)),
        name="sgu_branch",
    )(uv, ln_g.reshape(1, c), ln_b.reshape(1, c), w_s, bs_b)


TQ = 128
TK = 128
SB_PAIRS = 2
SB_KEY_TILES = 2


def _suffix_matrix():
    r = np.arange(2 * TK)[:, None] % TK
    c = np.arange(2 * TK)[None, :]
    return jnp.asarray(((c >= TK) | (r > c)).astype(np.float32), dtype=bf16)


def _sb_kernel(q_ref, k_ref, v_ref, u2_ref, o_ref, qm_ref, c_ref, acc_ref):
    i = pl.program_id(1)
    lane = lax.broadcasted_iota(jnp.int32, (TQ, LANES), 1)
    low = lane < SB_HEAD_DIM
    row2 = lax.broadcasted_iota(jnp.int32, (2 * TQ, TK), 0) & (TQ - 1)
    col2 = lax.broadcasted_iota(jnp.int32, (2 * TQ, TK), 1)
    causal2 = col2 < row2

    for p in range(SB_PAIRS):
        q = q_ref[:, p * LANES:(p + 1) * LANES]
        qm_ref[p, 0:TQ, :] = jnp.where(low, q, jnp.zeros_like(q))
        qm_ref[p, TQ:2 * TQ, :] = jnp.where(low, jnp.zeros_like(q), q)

    def tiles(combos, carries):
        log_betas, log_1mbs = [], []
        for p, j, _, diag in combos:
            r0 = pl.multiple_of(j * TK, TK)
            kblk = k_ref[pl.ds(r0, TK), p * LANES:(p + 1) * LANES]
            z = lax.dot_general(qm_ref[p], kblk, (((1,), (1,)), ((), ())), preferred_element_type=f32)
            sp = jnp.log(1.0 + jnp.exp(-jnp.abs(z)))
            log_beta = jnp.minimum(z, 0.0) - sp
            log_1mb = log_beta - z
            if diag:
                log_1mb = jnp.where(causal2, log_1mb, 0.0)
            log_betas.append(log_beta)
            log_1mbs.append(log_1mb)
        l1 = jnp.concatenate(log_1mbs, axis=0)
        hi = l1.astype(bf16)
        lo = (l1 - hi.astype(f32)).astype(bf16)
        t_all = jnp.dot(jnp.concatenate([hi, lo], axis=1), u2_ref[...], preferred_element_type=f32)
        out = {}
        for n, (p, j, valid, diag) in enumerate(combos):
            t = t_all[n * 2 * TQ:(n + 1) * 2 * TQ]
            pv, carry = out.get(p, (None, carries[p]))
            x = log_betas[n] + t[:, :TK]
            a = jnp.exp(x if carry is None else x + carry)
            if diag:
                a = jnp.where(causal2, a, 0.0)
            a = a.astype(bf16)
            r0 = pl.multiple_of(j * TK, TK)
            vblk = v_ref[pl.ds(r0, TK), p * LANES:(p + 1) * LANES]
            keep0, keep1 = low, jnp.logical_not(low)
            if valid is not None:
                keep0, keep1 = jnp.logical_and(keep0, valid), jnp.logical_and(keep1, valid)
            vz = jnp.zeros_like(vblk)
            vcat = jnp.concatenate([jnp.where(keep0, vblk, vz), jnp.where(keep1, vblk, vz)], axis=0)
            inc = jnp.dot(jnp.concatenate([a[0:TQ], a[TQ:2 * TQ]], axis=1), vcat,
                          preferred_element_type=f32)
            tot = t[:, TK:]
            out[p] = (inc if pv is None else pv + inc, tot if carry is None else carry + tot)
        return out

    def carry_max(out):
        cm = None
        for p in range(SB_PAIRS):
            cm = out[p][1] if cm is None else jnp.maximum(cm, out[p][1])
        return jnp.max(cm)

    def key_tiles(j, n, always_first):
        return [(jnp.maximum(j - t, 0), None if (t == 0 and always_first) else j - t >= 0)
                for t in range(n)]

    combos = []
    for p in range(SB_PAIRS):
        combos.append((p, i, None, True))
        combos += [(p, j, valid, False) for j, valid in key_tiles(i - 1, SB_KEY_TILES, False)]
    out = tiles(combos, {p: None for p in range(SB_PAIRS)})
    for p in range(SB_PAIRS):
        acc_ref[:, p * LANES:(p + 1) * LANES] = out[p][0]
        c_ref[p] = out[p][1]

    def cond(st):
        j, cmax = st
        return jnp.logical_and(j >= 0, cmax > F32_EXP_ZERO_BELOW)

    def body(st):
        j, _ = st
        combos = []
        for p in range(SB_PAIRS):
            combos += [(p, jt, valid, False) for jt, valid in key_tiles(j, SB_KEY_TILES, True)]
        out = tiles(combos, {p: c_ref[p] for p in range(SB_PAIRS)})
        for p in range(SB_PAIRS):
            acc_ref[:, p * LANES:(p + 1) * LANES] += out[p][0]
            c_ref[p] = out[p][1]
        return j - SB_KEY_TILES, carry_max(out)

    lax.while_loop(cond, body, (i - 1 - SB_KEY_TILES, carry_max(out)))
    o_ref[...] = acc_ref[...].astype(o_ref.dtype)


def sb_attention(pc, n_heads):
    s = pc.shape[0]
    w = SB_PAIRS * LANES
    ngrp = n_heads * SB_HEAD_DIM // w
    return pl.pallas_call(
        _sb_kernel,
        grid=(ngrp, s // TQ),
        in_specs=[pl.BlockSpec((TQ, w), lambda g, i: (i, g)),
                  pl.BlockSpec((s, w), lambda g, i: (0, ngrp + g)),
                  pl.BlockSpec((s, w), lambda g, i: (0, 2 * ngrp + g)),
                  pl.BlockSpec((2 * TK, 2 * TK), lambda g, i: (0, 0))],
        out_specs=pl.BlockSpec((TQ, w), lambda g, i: (i, g)),
        out_shape=jax.ShapeDtypeStruct((s, ngrp * w), bf16),
        scratch_shapes=[pltpu.VMEM((SB_PAIRS, 2 * TQ, LANES), bf16),
                        pltpu.VMEM((SB_PAIRS, 2 * TQ, TK), f32),
                        pltpu.VMEM((TQ, w), f32)],
        compiler_params=_cparams(("parallel", "arbitrary")),
        name="sb_attention",
    )(pc, pc, pc, _suffix_matrix())


def _merge_kernel(a_ref, b_ref, c_ref, wa_ref, wb_ref, wc_ref, g0_ref, g1_ref, g2_ref, o_ref):
    ya = jnp.dot(a_ref[...], wa_ref[...].astype(bf16), preferred_element_type=f32)
    yb = jnp.dot(b_ref[...], wb_ref[...].astype(bf16), preferred_element_type=f32)
    yc = jnp.dot(c_ref[...], wc_ref[...].astype(bf16), preferred_element_type=f32)
    mixed = (g0_ref[...].astype(f32) * ya + g1_ref[...].astype(f32) * yb
             + g2_ref[...].astype(f32) * yc)
    o_ref[...] = mixed.astype(o_ref.dtype)


def gated_merge(a, b, c, wa, wb, wc, layer, gates, tm=1024, tn=512):
    s, kd = a.shape
    tm = min(tm, s)
    d = wa.shape[2]
    act = pl.BlockSpec((tm, kd), lambda i, j: (i, 0))
    wsp = pl.BlockSpec((None, kd, tn), lambda i, j: (layer, 0, j))
    gsp = lambda b_: pl.BlockSpec((tm, tn), lambda i, j, b_=b_: (i, j + b_ * (d // tn)))
    return pl.pallas_call(
        _merge_kernel,
        grid=(s // tm, d // tn),
        in_specs=[act, act, act, wsp, wsp, wsp, gsp(0), gsp(1), gsp(2)],
        out_specs=pl.BlockSpec((tm, tn), lambda i, j: (i, j)),
        out_shape=jax.ShapeDtypeStruct((s, d), bf16),
        compiler_params=_cparams(("parallel", "parallel")),
        name="gated_merge",
    )(a, b, c, wa, wb, wc, gates, gates, gates)


def kernel(x, attn_norm_g, w_in, b_gate, conv_w, conv_b, conv_ln_g, conv_ln_b, sgu_ln_g, sgu_ln_b,
           sgu_w, sgu_b, w_out_conv, w_out_sgu, w_out_sb, w_o, mlp_norm_g, w_ff1, w_ff2, final_norm_g):
    bsz, seq, d = x.shape
    depth = w_in.shape[0]
    conv_dim = conv_w.shape[2]
    sgu_dim = sgu_ln_g.shape[1]
    sb_dim = w_out_sb.shape[1]
    n_heads = sb_dim // SB_HEAD_DIM
    a_cols, b_cols, c_cols = 2 * conv_dim, 2 * sgu_dim, 3 * sb_dim
    g_cols = N_BRANCH * d
    tn = 512
    tw = 1024

    outs = []
    for bi in range(bsz):
        xs = x[bi]
        for l in range(depth):
            h = rmsnorm(xs, attn_norm_g[l], bf16)
            z = matmul(h, [(w_in, l, 0), (w_in, l, conv_dim)], conv_dim, _ep_glu, f32, tn=tn,
                       chunk=MXU_COLS, name="proj_glu")
            uv = matmul(h, [(w_in, l, a_cols)], b_cols, _ep_gelu, bf16, tn=tw, name="proj_gelu")
            pc = matmul(h, [(w_in, l, a_cols + b_cols)], c_cols,
                        functools.partial(_ep_qkv, q_tiles=sb_dim // tw), bf16, tn=tw, name="proj_qkv")
            gates = matmul(h, [(w_in, l, a_cols + b_cols + c_cols)], g_cols, _ep_gates, bf16,
                           extras=[(b_gate[l].reshape(1, g_cols), 0, "row")], tn=tw, name="proj_gates")

            ya = conv_branch(z, conv_w[l], conv_b[l], conv_ln_g[l], conv_ln_b[l])
            yb = sgu_branch(uv, sgu_ln_g[l], sgu_ln_b[l], sgu_w[l], sgu_b[l])
            yc = sb_attention(pc, n_heads)

            mixed = gated_merge(ya, yb, yc, w_out_conv, w_out_sgu, w_out_sb, l, gates)
            xs = matmul(mixed, [(w_o, l, 0)], d, _ep_residual, f32,
                        extras=[(xs, 0, "tile")], tn=tw, name="wo_residual")

            h2 = rmsnorm(xs, mlp_norm_g[l], bf16)
            f = matmul(h2, [(w_ff1, l, 0)], w_ff1.shape[2], _ep_relu2, bf16, tn=tw, name="ff1")
            xs = matmul(f, [(w_ff2, l, 0)], d, _ep_residual, f32,
                        extras=[(xs, 0, "tile")], tn=tw, name="ff2_residual")
        outs.append(rmsnorm(xs, final_norm_g, f32))
    return jnp.stack(outs, axis=0)
```

```python
import functools
import math

import numpy as np
import jax
import jax.numpy as jnp
from jax import lax
from jax.experimental import pallas as pl
from jax.experimental.pallas import tpu as pltpu

EPS = 1e-6
CONV_WIDTH = 31
CHUNK = 128
SGU_GROUPS = 8
SB_HEAD_DIM = 64
N_BRANCH = 3

LANES = 128
SUBLANES = 8
MXU_COLS = 256
VMEM_LIMIT = 56 * 1024 * 1024
F32_EXP_ZERO_BELOW = -110.0

f32 = jnp.float32
bf16 = jnp.bfloat16


def _cparams(sem):
    return pltpu.CompilerParams(dimension_semantics=sem, vmem_limit_bytes=VMEM_LIMIT)


def _rmsnorm_kernel(x_ref, g_ref, o_ref):
    x = x_ref[...]
    var = jnp.mean(x * x, axis=-1, keepdims=True)
    o_ref[...] = (x * lax.rsqrt(var + EPS) * g_ref[...]).astype(o_ref.dtype)


def rmsnorm(x, g, out_dtype, tm=512):
    s, d = x.shape
    return pl.pallas_call(
        _rmsnorm_kernel,
        grid=(s // tm,),
        in_specs=[pl.BlockSpec((tm, d), lambda i: (i, 0)),
                  pl.BlockSpec((1, d), lambda i: (0, 0))],
        out_specs=pl.BlockSpec((tm, d), lambda i: (i, 0)),
        out_shape=jax.ShapeDtypeStruct((s, d), out_dtype),
        compiler_params=_cparams(("parallel",)),
        name="rmsnorm",
    )(x, g.reshape(1, d))


def _mm_kernel(*refs, n_w, n_extra, nk, epilogue, chunk, row_chunk, j_axis):
    a_ref = refs[0]
    w_refs = refs[1:1 + n_w]
    extra_refs = refs[1 + n_w:1 + n_w + n_extra]
    o_ref = refs[1 + n_w + n_extra]
    j = pl.program_id(j_axis)

    if nk == 1:
        tm, tn = o_ref.shape
        for c0 in range(0, tn, chunk or tn):
            cols = slice(c0, c0 + (chunk or tn))
            wcs = [w[:, cols].astype(bf16) for w in w_refs]
            for r0 in range(0, tm, row_chunk or tm):
                rows = slice(r0, r0 + (row_chunk or tm))
                accs = [jnp.dot(a_ref[rows, :], wc, preferred_element_type=f32) for wc in wcs]
                exs = [e[:, cols] if e.shape[0] == 1 else e[rows, cols] for e in extra_refs]
                o_ref[rows, cols] = epilogue(accs, exs, j).astype(o_ref.dtype)
        return

    k = pl.program_id(2)

    @pl.when(k == 0)
    def _():
        o_ref[...] = extra_refs[0][...]

    o_ref[...] += jnp.dot(a_ref[...], w_refs[0][...].astype(bf16), preferred_element_type=f32)


def matmul(a, ws, n_out, epilogue, out_dtype, extras=(), tm=1024, tn=512, tk=2048, chunk=None,
           row_chunk=None, cols_outer=False, name="mm"):
    m, kdim = a.shape
    tm = min(tm, m)
    tk = min(tk, kdim)
    nk = kdim // tk
    if cols_outer:
        grid = (n_out // tn, m // tm, nk)
        ix = lambda f: (lambda j, i, k: f(i, j, k))
    else:
        grid = (m // tm, n_out // tn, nk)
        ix = lambda f: f
    in_specs = [pl.BlockSpec((tm, tk), ix(lambda i, j, k: (i, k)))]
    args = [a]
    for w, layer, off in ws:
        ob = off // tn
        in_specs.append(pl.BlockSpec((None, tk, tn),
                                     ix(lambda i, j, k, ob=ob, layer=layer: (layer, k, j + ob))))
        args.append(w)
    for e, off, kind in extras:
        ob = off // tn
        if kind == "tile":
            in_specs.append(pl.BlockSpec((tm, tn), ix(lambda i, j, k, ob=ob: (i, j + ob))))
        else:
            in_specs.append(pl.BlockSpec((1, tn), ix(lambda i, j, k, ob=ob: (0, j + ob))))
        args.append(e)
    if nk > 1:
        assert epilogue is _ep_residual and len(ws) == 1 and out_dtype == f32
    kern = functools.partial(_mm_kernel, n_w=len(ws), n_extra=len(extras), nk=nk, epilogue=epilogue,
                             chunk=chunk, row_chunk=row_chunk, j_axis=0 if cols_outer else 1)
    return pl.pallas_call(
        kern,
        grid=grid,
        in_specs=in_specs,
        out_specs=pl.BlockSpec((tm, tn), ix(lambda i, j, k: (i, j))),
        out_shape=jax.ShapeDtypeStruct((m, n_out), out_dtype),
        compiler_params=_cparams(("parallel", "parallel", "arbitrary")),
        name=name,
    )(*args)


def _ep_glu(accs, extras, j):
    return accs[0] * jax.nn.sigmoid(accs[1])


def _ep_gelu(accs, extras, j):
    return jax.nn.gelu(accs[0])


def _ep_qkv(accs, extras, j, q_tiles):
    scale = jnp.where(j < q_tiles, 1.0 / math.sqrt(SB_HEAD_DIM), 1.0).astype(f32)
    return accs[0] * scale


def _ep_gates(accs, extras, j):
    return jax.nn.sigmoid(accs[0] + extras[0])


def _ep_residual(accs, extras, j):
    return extras[0] + accs[0]


def _ep_relu2(accs, extras, j):
    return jnp.square(jnp.maximum(accs[0], 0.0))


HALO = 32
CONV_BASE = HALO - (CONV_WIDTH - 1)
CONV_TAP_GROUPS = -(-CONV_WIDTH // SUBLANES)
CONV_SHIFT_ROWS = (CONV_TAP_GROUPS - 1) * SUBLANES


def _conv_kernel(z_ref, halo_ref, cw_ref, cb_ref, g_ref, b_ref, o_ref, buf_ref, sh_ref, y_ref, *, tm, c):
    i = pl.program_id(0)
    halo = halo_ref[...]
    buf_ref[0:HALO, :] = jnp.where(i == 0, jnp.zeros_like(halo), halo)
    buf_ref[HALO:HALO + tm, :] = z_ref[...]

    def chan_block(cb, carry):
        c0 = pl.multiple_of(cb * LANES, LANES)
        for s in range(SUBLANES):
            n = tm + ((CONV_WIDTH - 1 - s) // SUBLANES) * SUBLANES
            sh_ref[s, 0:n, :] = buf_ref[pl.ds(CONV_BASE + s, n), pl.ds(c0, LANES)]
        acc = jnp.zeros((tm, LANES), f32)
        for w in range(CONV_WIDTH):
            a, s = divmod(w, SUBLANES)
            acc = acc + sh_ref[s, a * SUBLANES:a * SUBLANES + tm, :] * cw_ref[pl.ds(w, 1), pl.ds(c0, LANES)]
        y_ref[:, pl.ds(c0, LANES)] = acc
        return carry

    lax.fori_loop(0, c // LANES, chan_block, 0)
    y = y_ref[...] + cb_ref[...]
    mu = jnp.mean(y, axis=-1, keepdims=True)
    yc = y - mu
    var = jnp.mean(yc * yc, axis=-1, keepdims=True)
    yn = yc * lax.rsqrt(var + EPS) * g_ref[...] + b_ref[...]
    o_ref[...] = (yn * jax.nn.sigmoid(yn)).astype(o_ref.dtype)


def conv_branch(z, conv_w, conv_b, ln_g, ln_b, tm=256):
    s, c = z.shape
    cwp = jnp.zeros((HALO, c), f32).at[:CONV_WIDTH].set(conv_w)
    row = lambda v: v.reshape(1, c)
    kern = functools.partial(_conv_kernel, tm=tm, c=c)
    return pl.pallas_call(
        kern,
        grid=(s // tm,),
        in_specs=[pl.BlockSpec((tm, c), lambda i: (i, 0)),
                  pl.BlockSpec((HALO, c), lambda i: (jnp.maximum(i * (tm // HALO) - 1, 0), 0)),
                  pl.BlockSpec((HALO, c), lambda i: (0, 0)),
                  pl.BlockSpec((1, c), lambda i: (0, 0)),
                  pl.BlockSpec((1, c), lambda i: (0, 0)),
                  pl.BlockSpec((1, c), lambda i: (0, 0))],
        out_specs=pl.BlockSpec((tm, c), lambda i: (i, 0)),
        out_shape=jax.ShapeDtypeStruct((s, c), bf16),
        scratch_shapes=[pltpu.VMEM((HALO + tm, c), f32),
                        pltpu.VMEM((SUBLANES, tm + CONV_SHIFT_ROWS, LANES), f32),
                        pltpu.VMEM((tm, c), f32)],
        compiler_params=_cparams(("parallel",)),
        name="conv_branch",
    )(z, z, cwp, row(conv_b), row(ln_g), row(ln_b))


def _sgu_kernel(uv_ref, g_ref, b_ref, w_ref, bs_ref, o_ref, *, tm, c):
    v = uv_ref[:, c:2 * c].astype(f32)
    mu = jnp.mean(v, axis=-1, keepdims=True)
    vc = v - mu
    var = jnp.mean(vc * vc, axis=-1, keepdims=True)
    vn = (vc * lax.rsqrt(var + EPS) * g_ref[...] + b_ref[...]).astype(bf16)
    row = lax.broadcasted_iota(jnp.int32, (CHUNK, CHUNK), 0)
    col = lax.broadcasted_iota(jnp.int32, (CHUNK, CHUNK), 1)
    tril = col <= row
    gd = c // SGU_GROUPS
    for g in range(SGU_GROUPS):
        wg = jnp.where(tril, w_ref[g], 0.0).astype(bf16)
        for ch in range(tm // CHUNK):
            r0 = ch * CHUNK
            s = jnp.dot(wg, vn[r0:r0 + CHUNK, g * gd:(g + 1) * gd], preferred_element_type=f32)
            s = s + bs_ref[g]
            u = uv_ref[r0:r0 + CHUNK, g * gd:(g + 1) * gd].astype(f32)
            o_ref[r0:r0 + CHUNK, g * gd:(g + 1) * gd] = (u * s).astype(o_ref.dtype)


def sgu_branch(uv, ln_g, ln_b, w_s, b_s, tm=256):
    s, c2 = uv.shape
    c = c2 // 2
    bs_b = jnp.broadcast_to(b_s[:, :, None], (SGU_GROUPS, CHUNK, c // SGU_GROUPS))
    kern = functools.partial(_sgu_kernel, tm=tm, c=c)
    return pl.pallas_call(
        kern,
        grid=(s // tm,),
        in_specs=[pl.BlockSpec((tm, c2), lambda i: (i, 0)),
                  pl.BlockSpec((1, c), lambda i: (0, 0)),
                  pl.BlockSpec((1, c), lambda i: (0, 0)),
                  pl.BlockSpec((SGU_GROUPS, CHUNK, CHUNK), lambda i: (0, 0, 0)),
                  pl.BlockSpec((SGU_GROUPS, CHUNK, c // SGU_GROUPS), lambda i: (0, 0, 0))],
        out_specs=pl.BlockSpec((tm, c), lambda i: (i, 0)),
        out_shape=jax.ShapeDtypeStruct((s, c), bf16),
        compiler_params=_cparams(("parallel",)),
        name="sgu_branch",
    )(uv, ln_g.reshape(1, c), ln_b.reshape(1, c), w_s, bs_b)


TQ = 128
TK = 128
SB_PAIRS = 4
SB_KEY_TILES = 2


def _suffix_matrix():
    r = np.arange(2 * TK)[:, None] % TK
    c = np.arange(2 * TK)[None, :]
    return jnp.asarray(((c >= TK) | (r > c)).astype(np.float32), dtype=bf16)


def _sb_kernel(q_ref, k_ref, v_ref, u2_ref, o_ref, qm_ref, c_ref, acc_ref):
    i = pl.program_id(1)
    lane = lax.broadcasted_iota(jnp.int32, (TQ, LANES), 1)
    low = lane < SB_HEAD_DIM
    row2 = lax.broadcasted_iota(jnp.int32, (2 * TQ, TK), 0) & (TQ - 1)
    col2 = lax.broadcasted_iota(jnp.int32, (2 * TQ, TK), 1)
    causal2 = col2 < row2

    for p in range(SB_PAIRS):
        q = q_ref[:, p * LANES:(p + 1) * LANES]
        qm_ref[p, 0:TQ, :] = jnp.where(low, q, jnp.zeros_like(q))
        qm_ref[p, TQ:2 * TQ, :] = jnp.where(low, jnp.zeros_like(q), q)

    def tiles(combos, carries):
        log_betas, log_1mbs = [], []
        for p, j, _, diag in combos:
            r0 = pl.multiple_of(j * TK, TK)
            kblk = k_ref[pl.ds(r0, TK), p * LANES:(p + 1) * LANES]
            z = lax.dot_general(qm_ref[p], kblk, (((1,), (1,)), ((), ())), preferred_element_type=f32)
            sp = jnp.log(1.0 + jnp.exp(-jnp.abs(z)))
            log_beta = jnp.minimum(z, 0.0) - sp
            log_1mb = log_beta - z
            if diag:
                log_1mb = jnp.where(causal2, log_1mb, 0.0)
            log_betas.append(log_beta)
            log_1mbs.append(log_1mb)
        l1 = jnp.concatenate(log_1mbs, axis=0)
        hi = l1.astype(bf16)
        lo = (l1 - hi.astype(f32)).astype(bf16)
        t_all = jnp.dot(jnp.concatenate([hi, lo], axis=1), u2_ref[...], preferred_element_type=f32)
        out = {}
        for n, (p, j, valid, diag) in enumerate(combos):
            t = t_all[n * 2 * TQ:(n + 1) * 2 * TQ]
            pv, carry = out.get(p, (None, carries[p]))
            x = log_betas[n] + t[:, :TK]
            a = jnp.exp(x if carry is None else x + carry)
            if diag:
                a = jnp.where(causal2, a, 0.0)
            a = a.astype(bf16)
            r0 = pl.multiple_of(j * TK, TK)
            vblk = v_ref[pl.ds(r0, TK), p * LANES:(p + 1) * LANES]
            keep0, keep1 = low, jnp.logical_not(low)
            if valid is not None:
                keep0, keep1 = jnp.logical_and(keep0, valid), jnp.logical_and(keep1, valid)
            vz = jnp.zeros_like(vblk)
            vcat = jnp.concatenate([jnp.where(keep0, vblk, vz), jnp.where(keep1, vblk, vz)], axis=0)
            inc = jnp.dot(jnp.concatenate([a[0:TQ], a[TQ:2 * TQ]], axis=1), vcat,
                          preferred_element_type=f32)
            tot = t[:, TK:]
            out[p] = (inc if pv is None else pv + inc, tot if carry is None else carry + tot)
        return out

    def carry_max(out):
        cm = None
        for p in range(SB_PAIRS):
            cm = out[p][1] if cm is None else jnp.maximum(cm, out[p][1])
        return jnp.max(cm)

    def key_tiles(j, n, always_first):
        return [(jnp.maximum(j - t, 0), None if (t == 0 and always_first) else j - t >= 0)
                for t in range(n)]

    combos = []
    for p in range(SB_PAIRS):
        combos.append((p, i, None, True))
        combos += [(p, j, valid, False) for j, valid in key_tiles(i - 1, SB_KEY_TILES, False)]
    out = tiles(combos, {p: None for p in range(SB_PAIRS)})
    for p in range(SB_PAIRS):
        acc_ref[:, p * LANES:(p + 1) * LANES] = out[p][0]
        c_ref[p] = out[p][1]

    def cond(st):
        j, cmax = st
        return jnp.logical_and(j >= 0, cmax > F32_EXP_ZERO_BELOW)

    def body(st):
        j, _ = st
        combos = []
        for p in range(SB_PAIRS):
            combos += [(p, jt, valid, False) for jt, valid in key_tiles(j, SB_KEY_TILES, True)]
        out = tiles(combos, {p: c_ref[p] for p in range(SB_PAIRS)})
        for p in range(SB_PAIRS):
            acc_ref[:, p * LANES:(p + 1) * LANES] += out[p][0]
            c_ref[p] = out[p][1]
        return j - SB_KEY_TILES, carry_max(out)

    lax.while_loop(cond, body, (i - 1 - SB_KEY_TILES, carry_max(out)))
    o_ref[...] = acc_ref[...].astype(o_ref.dtype)


def sb_attention(pc, n_heads):
    s = pc.shape[0]
    w = SB_PAIRS * LANES
    ngrp = n_heads * SB_HEAD_DIM // w
    return pl.pallas_call(
        _sb_kernel,
        grid=(ngrp, s // TQ),
        in_specs=[pl.BlockSpec((TQ, w), lambda g, i: (i, g)),
                  pl.BlockSpec((s, w), lambda g, i: (0, ngrp + g)),
                  pl.BlockSpec((s, w), lambda g, i: (0, 2 * ngrp + g)),
                  pl.BlockSpec((2 * TK, 2 * TK), lambda g, i: (0, 0))],
        out_specs=pl.BlockSpec((TQ, w), lambda g, i: (i, g)),
        out_shape=jax.ShapeDtypeStruct((s, ngrp * w), bf16),
        scratch_shapes=[pltpu.VMEM((SB_PAIRS, 2 * TQ, LANES), bf16),
                        pltpu.VMEM((SB_PAIRS, 2 * TQ, TK), f32),
                        pltpu.VMEM((TQ, w), f32)],
        compiler_params=_cparams(("parallel", "arbitrary")),
        name="sb_attention",
    )(pc, pc, pc, _suffix_matrix())


def _merge_kernel(a_ref, b_ref, c_ref, wa_ref, wb_ref, wc_ref, g0_ref, g1_ref, g2_ref, o_ref):
    ya = jnp.dot(a_ref[...], wa_ref[...].astype(bf16), preferred_element_type=f32)
    yb = jnp.dot(b_ref[...], wb_ref[...].astype(bf16), preferred_element_type=f32)
    yc = jnp.dot(c_ref[...], wc_ref[...].astype(bf16), preferred_element_type=f32)
    mixed = (g0_ref[...].astype(f32) * ya + g1_ref[...].astype(f32) * yb
             + g2_ref[...].astype(f32) * yc)
    o_ref[...] = mixed.astype(o_ref.dtype)


def gated_merge(a, b, c, wa, wb, wc, layer, gates, tm=512, tn=1024):
    s, kd = a.shape
    tm = min(tm, s)
    d = wa.shape[2]
    act = pl.BlockSpec((tm, kd), lambda j, i: (i, 0))
    wsp = pl.BlockSpec((None, kd, tn), lambda j, i: (layer, 0, j))
    gsp = lambda b_: pl.BlockSpec((tm, tn), lambda j, i, b_=b_: (i, j + b_ * (d // tn)))
    return pl.pallas_call(
        _merge_kernel,
        grid=(d // tn, s // tm),
        in_specs=[act, act, act, wsp, wsp, wsp, gsp(0), gsp(1), gsp(2)],
        out_specs=pl.BlockSpec((tm, tn), lambda j, i: (i, j)),
        out_shape=jax.ShapeDtypeStruct((s, d), bf16),
        compiler_params=_cparams(("parallel", "parallel")),
        name="gated_merge",
    )(a, b, c, wa, wb, wc, gates, gates, gates)


def kernel(x, attn_norm_g, w_in, b_gate, conv_w, conv_b, conv_ln_g, conv_ln_b, sgu_ln_g, sgu_ln_b,
           sgu_w, sgu_b, w_out_conv, w_out_sgu, w_out_sb, w_o, mlp_norm_g, w_ff1, w_ff2, final_norm_g):
    bsz, seq, d = x.shape
    depth = w_in.shape[0]
    conv_dim = conv_w.shape[2]
    sgu_dim = sgu_ln_g.shape[1]
    sb_dim = w_out_sb.shape[1]
    n_heads = sb_dim // SB_HEAD_DIM
    a_cols, b_cols, c_cols = 2 * conv_dim, 2 * sgu_dim, 3 * sb_dim
    g_cols = N_BRANCH * d
    tn = 512
    tw = 1024

    outs = []
    for bi in range(bsz):
        xs = x[bi]
        for l in range(depth):
            h = rmsnorm(xs, attn_norm_g[l], bf16)
            z = matmul(h, [(w_in, l, 0), (w_in, l, conv_dim)], conv_dim, _ep_glu, f32, tn=tn,
                       chunk=MXU_COLS, name="proj_glu")
            uv = matmul(h, [(w_in, l, a_cols)], b_cols, _ep_gelu, bf16, tn=tw, row_chunk=256,
                        name="proj_gelu")
            pc = matmul(h, [(w_in, l, a_cols + b_cols)], c_cols,
                        functools.partial(_ep_qkv, q_tiles=sb_dim // tw), bf16, tn=tw, name="proj_qkv")
            gates = matmul(h, [(w_in, l, a_cols + b_cols + c_cols)], g_cols, _ep_gates, bf16,
                           extras=[(b_gate[l].reshape(1, g_cols), 0, "row")], tn=tw, row_chunk=256,
                           name="proj_gates")

            ya = conv_branch(z, conv_w[l], conv_b[l], conv_ln_g[l], conv_ln_b[l])
            yb = sgu_branch(uv, sgu_ln_g[l], sgu_ln_b[l], sgu_w[l], sgu_b[l])
            yc = sb_attention(pc, n_heads)

            mixed = gated_merge(ya, yb, yc, w_out_conv, w_out_sgu, w_out_sb, l, gates)
            xs = matmul(mixed, [(w_o, l, 0)], d, _ep_residual, f32,
                        extras=[(xs, 0, "tile")], tn=tw, cols_outer=True, name="wo_residual")

            h2 = rmsnorm(xs, mlp_norm_g[l], bf16)
            f = matmul(h2, [(w_ff1, l, 0)], w_ff1.shape[2], _ep_relu2, bf16, tn=tw, name="ff1")
            xs = matmul(f, [(w_ff2, l, 0)], d, _ep_residual, f32,
                        extras=[(xs, 0, "tile")], tn=tw, name="ff2_residual")
        outs.append(rmsnorm(xs, final_norm_g, f32))
    return jnp.stack(outs, axis=0)
```

```python
import functools
import math

import numpy as np
import jax
import jax.numpy as jnp
from jax import lax
from jax.experimental import pallas as pl
from jax.experimental.pallas import tpu as pltpu

EPS = 1e-6
CONV_WIDTH = 31
CHUNK = 128
SGU_GROUPS = 8
SB_HEAD_DIM = 64
N_BRANCH = 3

LANES = 128
SUBLANES = 8
MXU_COLS = 256
VMEM_LIMIT = 56 * 1024 * 1024
F32_EXP_ZERO_BELOW = -110.0

f32 = jnp.float32
bf16 = jnp.bfloat16


def _cparams(sem):
    return pltpu.CompilerParams(dimension_semantics=sem, vmem_limit_bytes=VMEM_LIMIT)


def _rmsnorm_kernel(x_ref, g_ref, o_ref):
    x = x_ref[...]
    var = jnp.mean(x * x, axis=-1, keepdims=True)
    o_ref[...] = (x * lax.rsqrt(var + EPS) * g_ref[...]).astype(o_ref.dtype)


def rmsnorm(x, g, out_dtype, tm=1024):
    s, d = x.shape
    return pl.pallas_call(
        _rmsnorm_kernel,
        grid=(s // tm,),
        in_specs=[pl.BlockSpec((tm, d), lambda i: (i, 0)),
                  pl.BlockSpec((1, d), lambda i: (0, 0))],
        out_specs=pl.BlockSpec((tm, d), lambda i: (i, 0)),
        out_shape=jax.ShapeDtypeStruct((s, d), out_dtype),
        compiler_params=_cparams(("parallel",)),
        name="rmsnorm",
    )(x, g.reshape(1, d))


def _mm_kernel(*refs, n_w, n_extra, nk, epilogue, chunk, row_chunk, j_axis, concat):
    a_ref = refs[0]
    w_refs = refs[1:1 + n_w]
    extra_refs = refs[1 + n_w:1 + n_w + n_extra]
    o_ref = refs[1 + n_w + n_extra]
    j = pl.program_id(j_axis)

    if nk == 1:
        tm, tn = o_ref.shape
        if concat:
            tnw = tn // n_w
            groups = [(slice(t * tnw, (t + 1) * tnw), [w[...].astype(bf16)])
                      for t, w in enumerate(w_refs)]
        else:
            groups = [(slice(c0, c0 + (chunk or tn)),
                       [w[:, c0:c0 + (chunk or tn)].astype(bf16) for w in w_refs])
                      for c0 in range(0, tn, chunk or tn)]
        for cols, wcs in groups:
            for r0 in range(0, tm, row_chunk or tm):
                rows = slice(r0, r0 + (row_chunk or tm))
                accs = [jnp.dot(a_ref[rows, :], wc, preferred_element_type=f32) for wc in wcs]
                exs = [e[:, cols] if e.shape[0] == 1 else e[rows, cols] for e in extra_refs]
                o_ref[rows, cols] = epilogue(accs, exs, j).astype(o_ref.dtype)
        return

    k = pl.program_id(2)

    @pl.when(k == 0)
    def _():
        o_ref[...] = extra_refs[0][...]

    o_ref[...] += jnp.dot(a_ref[...], w_refs[0][...].astype(bf16), preferred_element_type=f32)


def matmul(a, ws, n_out, epilogue, out_dtype, extras=(), tm=1024, tn=512, tk=2048, chunk=None,
           row_chunk=None, cols_outer=False, w_tiles=1, name="mm"):
    m, kdim = a.shape
    tm = min(tm, m)
    tk = min(tk, kdim)
    nk = kdim // tk
    if cols_outer:
        grid = (n_out // tn, m // tm, nk)
        ix = lambda f: (lambda j, i, k: f(i, j, k))
    else:
        grid = (m // tm, n_out // tn, nk)
        ix = lambda f: f
    in_specs = [pl.BlockSpec((tm, tk), ix(lambda i, j, k: (i, k)))]
    args = [a]
    if w_tiles > 1:
        (w, layer, off), = ws
        tnw = tn // w_tiles
        ws = [(w, layer, off + t * tnw) for t in range(w_tiles)]
    else:
        tnw = tn
    for w, layer, off in ws:
        ob = off // tnw
        in_specs.append(pl.BlockSpec(
            (None, tk, tnw),
            ix(lambda i, j, k, ob=ob, layer=layer: (layer, k, w_tiles * j + ob))))
        args.append(w)
    for e, off, kind in extras:
        ob = off // tn
        if kind == "tile":
            in_specs.append(pl.BlockSpec((tm, tn), ix(lambda i, j, k, ob=ob: (i, j + ob))))
        else:
            in_specs.append(pl.BlockSpec((1, tn), ix(lambda i, j, k, ob=ob: (0, j + ob))))
        args.append(e)
    if nk > 1:
        assert epilogue is _ep_residual and len(ws) == 1 and out_dtype == f32
    kern = functools.partial(_mm_kernel, n_w=len(ws), n_extra=len(extras), nk=nk, epilogue=epilogue,
                             chunk=chunk, row_chunk=row_chunk, j_axis=0 if cols_outer else 1,
                             concat=w_tiles > 1)
    return pl.pallas_call(
        kern,
        grid=grid,
        in_specs=in_specs,
        out_specs=pl.BlockSpec((tm, tn), ix(lambda i, j, k: (i, j))),
        out_shape=jax.ShapeDtypeStruct((m, n_out), out_dtype),
        compiler_params=_cparams(("parallel", "parallel", "arbitrary")),
        name=name,
    )(*args)


def _ep_glu(accs, extras, j):
    return accs[0] * jax.nn.sigmoid(accs[1])


def _ep_gelu(accs, extras, j):
    return jax.nn.gelu(accs[0])


def _ep_qkv(accs, extras, j, q_tiles):
    scale = jnp.where(j < q_tiles, 1.0 / math.sqrt(SB_HEAD_DIM), 1.0).astype(f32)
    return accs[0] * scale


def _ep_gates(accs, extras, j):
    return jax.nn.sigmoid(accs[0] + extras[0])


def _ep_residual(accs, extras, j):
    return extras[0] + accs[0]


def _ep_relu2(accs, extras, j):
    return jnp.square(jnp.maximum(accs[0], 0.0))


def _wo_norm_kernel(a_ref, w_ref, x_ref, g_ref, ox_ref, oh_ref, wb_ref):
    @pl.when(pl.program_id(0) == 0)
    def _():
        wb_ref[...] = w_ref[...].astype(bf16)

    x1 = x_ref[...] + jnp.dot(a_ref[...], wb_ref[...], preferred_element_type=f32)
    ox_ref[...] = x1
    var = jnp.mean(x1 * x1, axis=-1, keepdims=True)
    oh_ref[...] = (x1 * lax.rsqrt(var + EPS) * g_ref[...]).astype(oh_ref.dtype)


def wo_residual_norm(a, w, layer, x, g, tm=512):
    s, kd = a.shape
    d = w.shape[2]
    tm = min(tm, s)
    row_blk = lambda width: pl.BlockSpec((tm, width), lambda i: (i, 0))
    return pl.pallas_call(
        _wo_norm_kernel,
        grid=(s // tm,),
        in_specs=[row_blk(kd),
                  pl.BlockSpec((None, kd, d), lambda i: (layer, 0, 0), pipeline_mode=pl.Buffered(1)),
                  row_blk(d),
                  pl.BlockSpec((1, d), lambda i: (0, 0))],
        out_specs=[row_blk(d), row_blk(d)],
        out_shape=[jax.ShapeDtypeStruct((s, d), f32), jax.ShapeDtypeStruct((s, d), bf16)],
        scratch_shapes=[pltpu.VMEM((kd, d), bf16)],
        compiler_params=_cparams(("arbitrary",)),
        name="wo_residual_norm",
    )(a, w, x, g.reshape(1, d))


HALO = 32
CONV_BASE = HALO - (CONV_WIDTH - 1)
CONV_TAP_GROUPS = -(-CONV_WIDTH // SUBLANES)
CONV_SHIFT_ROWS = (CONV_TAP_GROUPS - 1) * SUBLANES


def _conv_kernel(z_ref, halo_ref, cw_ref, cb_ref, g_ref, b_ref, o_ref, buf_ref, sh_ref, y_ref, *, tm, c):
    i = pl.program_id(0)
    halo = halo_ref[...]
    buf_ref[0:HALO, :] = jnp.where(i == 0, jnp.zeros_like(halo), halo)
    buf_ref[HALO:HALO + tm, :] = z_ref[...]

    def chan_block(cb, carry):
        c0 = pl.multiple_of(cb * LANES, LANES)
        for s in range(SUBLANES):
            n = tm + ((CONV_WIDTH - 1 - s) // SUBLANES) * SUBLANES
            sh_ref[s, 0:n, :] = buf_ref[pl.ds(CONV_BASE + s, n), pl.ds(c0, LANES)]
        acc = jnp.zeros((tm, LANES), f32)
        for w in range(CONV_WIDTH):
            a, s = divmod(w, SUBLANES)
            acc = acc + sh_ref[s, a * SUBLANES:a * SUBLANES + tm, :] * cw_ref[pl.ds(w, 1), pl.ds(c0, LANES)]
        y_ref[:, pl.ds(c0, LANES)] = acc
        return carry

    lax.fori_loop(0, c // LANES, chan_block, 0)
    y = y_ref[...] + cb_ref[...]
    mu = jnp.mean(y, axis=-1, keepdims=True)
    yc = y - mu
    var = jnp.mean(yc * yc, axis=-1, keepdims=True)
    yn = yc * lax.rsqrt(var + EPS) * g_ref[...] + b_ref[...]
    o_ref[...] = (yn * jax.nn.sigmoid(yn)).astype(o_ref.dtype)


def conv_branch(z, conv_w, conv_b, ln_g, ln_b, tm=256):
    s, c = z.shape
    cwp = jnp.zeros((HALO, c), f32).at[:CONV_WIDTH].set(conv_w)
    row = lambda v: v.reshape(1, c)
    kern = functools.partial(_conv_kernel, tm=tm, c=c)
    return pl.pallas_call(
        kern,
        grid=(s // tm,),
        in_specs=[pl.BlockSpec((tm, c), lambda i: (i, 0)),
                  pl.BlockSpec((HALO, c), lambda i: (jnp.maximum(i * (tm // HALO) - 1, 0), 0)),
                  pl.BlockSpec((HALO, c), lambda i: (0, 0)),
                  pl.BlockSpec((1, c), lambda i: (0, 0)),
                  pl.BlockSpec((1, c), lambda i: (0, 0)),
                  pl.BlockSpec((1, c), lambda i: (0, 0))],
        out_specs=pl.BlockSpec((tm, c), lambda i: (i, 0)),
        out_shape=jax.ShapeDtypeStruct((s, c), bf16),
        scratch_shapes=[pltpu.VMEM((HALO + tm, c), f32),
                        pltpu.VMEM((SUBLANES, tm + CONV_SHIFT_ROWS, LANES), f32),
                        pltpu.VMEM((tm, c), f32)],
        compiler_params=_cparams(("parallel",)),
        name="conv_branch",
    )(z, z, cwp, row(conv_b), row(ln_g), row(ln_b))


def _sgu_kernel(uv_ref, g_ref, b_ref, w_ref, bs_ref, o_ref, *, tm, c):
    v = uv_ref[:, c:2 * c].astype(f32)
    mu = jnp.mean(v, axis=-1, keepdims=True)
    vc = v - mu
    var = jnp.mean(vc * vc, axis=-1, keepdims=True)
    vn = (vc * lax.rsqrt(var + EPS) * g_ref[...] + b_ref[...]).astype(bf16)
    row = lax.broadcasted_iota(jnp.int32, (CHUNK, CHUNK), 0)
    col = lax.broadcasted_iota(jnp.int32, (CHUNK, CHUNK), 1)
    tril = col <= row
    gd = c // SGU_GROUPS
    for g in range(SGU_GROUPS):
        wg = jnp.where(tril, w_ref[g], 0.0).astype(bf16)
        for ch in range(tm // CHUNK):
            r0 = ch * CHUNK
            s = jnp.dot(wg, vn[r0:r0 + CHUNK, g * gd:(g + 1) * gd], preferred_element_type=f32)
            s = s + bs_ref[g]
            u = uv_ref[r0:r0 + CHUNK, g * gd:(g + 1) * gd].astype(f32)
            o_ref[r0:r0 + CHUNK, g * gd:(g + 1) * gd] = (u * s).astype(o_ref.dtype)


def sgu_branch(uv, ln_g, ln_b, w_s, b_s, tm=512):
    s, c2 = uv.shape
    c = c2 // 2
    bs_b = jnp.broadcast_to(b_s[:, :, None], (SGU_GROUPS, CHUNK, c // SGU_GROUPS))
    kern = functools.partial(_sgu_kernel, tm=tm, c=c)
    return pl.pallas_call(
        kern,
        grid=(s // tm,),
        in_specs=[pl.BlockSpec((tm, c2), lambda i: (i, 0)),
                  pl.BlockSpec((1, c), lambda i: (0, 0)),
                  pl.BlockSpec((1, c), lambda i: (0, 0)),
                  pl.BlockSpec((SGU_GROUPS, CHUNK, CHUNK), lambda i: (0, 0, 0)),
                  pl.BlockSpec((SGU_GROUPS, CHUNK, c // SGU_GROUPS), lambda i: (0, 0, 0))],
        out_specs=pl.BlockSpec((tm, c), lambda i: (i, 0)),
        out_shape=jax.ShapeDtypeStruct((s, c), bf16),
        compiler_params=_cparams(("parallel",)),
        name="sgu_branch",
    )(uv, ln_g.reshape(1, c), ln_b.reshape(1, c), w_s, bs_b)


TQ = 128
TK = 128
SB_PAIRS = 4
SB_KEY_TILES = 2


def _suffix_matrix():
    r = np.arange(2 * TK)[:, None] % TK
    c = np.arange(2 * TK)[None, :]
    return jnp.asarray(((c >= TK) | (r > c)).astype(np.float32), dtype=bf16)


def _sb_kernel(q_ref, k_ref, v_ref, u2_ref, o_ref, qm_ref, c_ref, acc_ref):
    i = pl.program_id(1)
    lane = lax.broadcasted_iota(jnp.int32, (TQ, LANES), 1)
    low = lane < SB_HEAD_DIM
    row2 = lax.broadcasted_iota(jnp.int32, (2 * TQ, TK), 0) & (TQ - 1)
    col2 = lax.broadcasted_iota(jnp.int32, (2 * TQ, TK), 1)
    causal2 = col2 < row2

    for p in range(SB_PAIRS):
        q = q_ref[:, p * LANES:(p + 1) * LANES]
        qm_ref[p, 0:TQ, :] = jnp.where(low, q, jnp.zeros_like(q))
        qm_ref[p, TQ:2 * TQ, :] = jnp.where(low, jnp.zeros_like(q), q)

    def tiles(combos, carries):
        log_betas, log_1mbs = [], []
        for p, j, _, diag in combos:
            r0 = pl.multiple_of(j * TK, TK)
            kblk = k_ref[pl.ds(r0, TK), p * LANES:(p + 1) * LANES]
            z = lax.dot_general(qm_ref[p], kblk, (((1,), (1,)), ((), ())), preferred_element_type=f32)
            sp = jnp.log(1.0 + jnp.exp(-jnp.abs(z)))
            log_beta = jnp.minimum(z, 0.0) - sp
            log_1mb = log_beta - z
            if diag:
                log_1mb = jnp.where(causal2, log_1mb, 0.0)
            log_betas.append(log_beta)
            log_1mbs.append(log_1mb)
        l1 = jnp.concatenate(log_1mbs, axis=0)
        hi = l1.astype(bf16)
        lo = (l1 - hi.astype(f32)).astype(bf16)
        t_all = jnp.dot(jnp.concatenate([hi, lo], axis=1), u2_ref[...], preferred_element_type=f32)
        out = {}
        for n, (p, j, valid, diag) in enumerate(combos):
            t = t_all[n * 2 * TQ:(n + 1) * 2 * TQ]
            pv, carry = out.get(p, (None, carries[p]))
            x = log_betas[n] + t[:, :TK]
            a = jnp.exp(x if carry is None else x + carry)
            if diag:
                a = jnp.where(causal2, a, 0.0)
            a = a.astype(bf16)
            r0 = pl.multiple_of(j * TK, TK)
            vblk = v_ref[pl.ds(r0, TK), p * LANES:(p + 1) * LANES]
            keep0, keep1 = low, jnp.logical_not(low)
            if valid is not None:
                keep0, keep1 = jnp.logical_and(keep0, valid), jnp.logical_and(keep1, valid)
            vz = jnp.zeros_like(vblk)
            vcat = jnp.concatenate([jnp.where(keep0, vblk, vz), jnp.where(keep1, vblk, vz)], axis=0)
            inc = jnp.dot(jnp.concatenate([a[0:TQ], a[TQ:2 * TQ]], axis=1), vcat,
                          preferred_element_type=f32)
            tot = t[:, TK:]
            out[p] = (inc if pv is None else pv + inc, tot if carry is None else carry + tot)
        return out

    def carry_max(out):
        cm = None
        for p in range(SB_PAIRS):
            cm = out[p][1] if cm is None else jnp.maximum(cm, out[p][1])
        return jnp.max(cm)

    def key_tiles(j, n, always_first):
        return [(jnp.maximum(j - t, 0), None if (t == 0 and always_first) else j - t >= 0)
                for t in range(n)]

    combos = []
    for p in range(SB_PAIRS):
        combos.append((p, i, None, True))
        combos += [(p, j, valid, False) for j, valid in key_tiles(i - 1, SB_KEY_TILES, False)]
    out = tiles(combos, {p: None for p in range(SB_PAIRS)})
    for p in range(SB_PAIRS):
        acc_ref[:, p * LANES:(p + 1) * LANES] = out[p][0]
        c_ref[p] = out[p][1]

    def cond(st):
        j, cmax = st
        return jnp.logical_and(j >= 0, cmax > F32_EXP_ZERO_BELOW)

    def body(st):
        j, _ = st
        combos = []
        for p in range(SB_PAIRS):
            combos += [(p, jt, valid, False) for jt, valid in key_tiles(j, SB_KEY_TILES, True)]
        out = tiles(combos, {p: c_ref[p] for p in range(SB_PAIRS)})
        for p in range(SB_PAIRS):
            acc_ref[:, p * LANES:(p + 1) * LANES] += out[p][0]
            c_ref[p] = out[p][1]
        return j - SB_KEY_TILES, carry_max(out)

    lax.while_loop(cond, body, (i - 1 - SB_KEY_TILES, carry_max(out)))
    o_ref[...] = acc_ref[...].astype(o_ref.dtype)


def sb_attention(pc, n_heads):
    s = pc.shape[0]
    w = SB_PAIRS * LANES
    ngrp = n_heads * SB_HEAD_DIM // w
    return pl.pallas_call(
        _sb_kernel,
        grid=(ngrp, s // TQ),
        in_specs=[pl.BlockSpec((TQ, w), lambda g, i: (i, g)),
                  pl.BlockSpec((s, w), lambda g, i: (0, ngrp + g)),
                  pl.BlockSpec((s, w), lambda g, i: (0, 2 * ngrp + g)),
                  pl.BlockSpec((2 * TK, 2 * TK), lambda g, i: (0, 0))],
        out_specs=pl.BlockSpec((TQ, w), lambda g, i: (i, g)),
        out_shape=jax.ShapeDtypeStruct((s, ngrp * w), bf16),
        scratch_shapes=[pltpu.VMEM((SB_PAIRS, 2 * TQ, LANES), bf16),
                        pltpu.VMEM((SB_PAIRS, 2 * TQ, TK), f32),
                        pltpu.VMEM((TQ, w), f32)],
        compiler_params=_cparams(("parallel", "arbitrary")),
        name="sb_attention",
    )(pc, pc, pc, _suffix_matrix())


def _merge_kernel(a_ref, b_ref, c_ref, wa_ref, wb_ref, wc_ref, g0_ref, g1_ref, g2_ref, o_ref):
    ya = jnp.dot(a_ref[...], wa_ref[...].astype(bf16), preferred_element_type=f32)
    yb = jnp.dot(b_ref[...], wb_ref[...].astype(bf16), preferred_element_type=f32)
    yc = jnp.dot(c_ref[...], wc_ref[...].astype(bf16), preferred_element_type=f32)
    mixed = (g0_ref[...].astype(f32) * ya + g1_ref[...].astype(f32) * yb
             + g2_ref[...].astype(f32) * yc)
    o_ref[...] = mixed.astype(o_ref.dtype)


def gated_merge(a, b, c, wa, wb, wc, layer, gates, tm=512, tn=1024):
    s, kd = a.shape
    tm = min(tm, s)
    d = wa.shape[2]
    act = pl.BlockSpec((tm, kd), lambda j, i: (i, 0))
    wsp = pl.BlockSpec((None, kd, tn), lambda j, i: (layer, 0, j))
    gsp = lambda b_: pl.BlockSpec((tm, tn), lambda j, i, b_=b_: (i, j + b_ * (d // tn)))
    return pl.pallas_call(
        _merge_kernel,
        grid=(d // tn, s // tm),
        in_specs=[act, act, act, wsp, wsp, wsp, gsp(0), gsp(1), gsp(2)],
        out_specs=pl.BlockSpec((tm, tn), lambda j, i: (i, j)),
        out_shape=jax.ShapeDtypeStruct((s, d), bf16),
        compiler_params=_cparams(("parallel", "parallel")),
        name="gated_merge",
    )(a, b, c, wa, wb, wc, gates, gates, gates)


def kernel(x, attn_norm_g, w_in, b_gate, conv_w, conv_b, conv_ln_g, conv_ln_b, sgu_ln_g, sgu_ln_b,
           sgu_w, sgu_b, w_out_conv, w_out_sgu, w_out_sb, w_o, mlp_norm_g, w_ff1, w_ff2, final_norm_g):
    bsz, seq, d = x.shape
    depth = w_in.shape[0]
    conv_dim = conv_w.shape[2]
    sgu_dim = sgu_ln_g.shape[1]
    sb_dim = w_out_sb.shape[1]
    n_heads = sb_dim // SB_HEAD_DIM
    a_cols, b_cols, c_cols = 2 * conv_dim, 2 * sgu_dim, 3 * sb_dim
    g_cols = N_BRANCH * d
    tn = 512
    tw = 1024

    outs = []
    for bi in range(bsz):
        xs = x[bi]
        for l in range(depth):
            h = rmsnorm(xs, attn_norm_g[l], bf16)
            z = matmul(h, [(w_in, l, 0), (w_in, l, conv_dim)], conv_dim, _ep_glu, f32, tn=tw,
                       chunk=MXU_COLS, name="proj_glu")
            uv = matmul(h, [(w_in, l, a_cols)], b_cols, _ep_gelu, bf16, tn=2 * tw, chunk=1024,
                        row_chunk=256, name="proj_gelu")
            pc = matmul(h, [(w_in, l, a_cols + b_cols)], c_cols,
                        functools.partial(_ep_qkv, q_tiles=sb_dim // tw), bf16, tn=tw, name="proj_qkv")
            gates = matmul(h, [(w_in, l, a_cols + b_cols + c_cols)], g_cols, _ep_gates, bf16,
                           extras=[(b_gate[l].reshape(1, g_cols), 0, "row")], tn=2 * tw, w_tiles=2,
                           row_chunk=256, name="proj_gates")

            ya = conv_branch(z, conv_w[l], conv_b[l], conv_ln_g[l], conv_ln_b[l])
            yb = sgu_branch(uv, sgu_ln_g[l], sgu_ln_b[l], sgu_w[l], sgu_b[l])
            yc = sb_attention(pc, n_heads)

            mixed = gated_merge(ya, yb, yc, w_out_conv, w_out_sgu, w_out_sb, l, gates)
            xs, h2 = wo_residual_norm(mixed, w_o, l, xs, mlp_norm_g[l])
            f = matmul(h2, [(w_ff1, l, 0)], w_ff1.shape[2], _ep_relu2, bf16, tn=2 * tw,
                       chunk=1024, name="ff1")
            xs = matmul(f, [(w_ff2, l, 0)], d, _ep_residual, f32,
                        extras=[(xs, 0, "tile")], tn=tw, name="ff2_residual")
        outs.append(rmsnorm(xs, final_norm_g, f32))
    return jnp.stack(outs, axis=0)
```
